```python
import jax
import jax.numpy as jnp
from jax import lax
import numpy as np

D_MODEL = 2048
BATCH = 4
SEQ = 4096
DEPTH = 1
DEC_BATCH = 32
DEC_SEQ = 4
PAST_LEN = 16384
PAGE_SIZE = 128

GLA_HEADS = 4
GLA_DV = D_MODEL // 8
GLA_DK = GLA_DV // 2
GLA_GATE_RANK = 16
GLA_TAU = 16.0
GLA_CHUNK = 64
SB_HEADS = 8
SB_DH = D_MODEL // 16
SB_BLOCK = 128
SB_BIAS_INIT = -8.0
GLA_QK_W = GLA_HEADS * GLA_DK
GLA_V_W = GLA_HEADS * GLA_DV
SB_W = SB_HEADS * SB_DH
MIX_W = GLA_V_W + SB_W
IN_W = 2 * GLA_QK_W + 2 * GLA_V_W + GLA_GATE_RANK + 3 * SB_W
N_EXPERTS = 64
TOP_K = 8
N_GROUPS = 8
TOPK_GROUPS = 4
D_EXPERT = D_MODEL // 4
D_SHARED = D_MODEL // 4
ROUTED_SCALE = 2.5
MOE_BLOCK = 256
EPS = 1e-6

kernel_name = 'hybrid_gla_stickbreak_moe_step'


def rmsnorm(x, g):
    xf = x.astype(jnp.float32)
    y = xf * lax.rsqrt(jnp.mean(xf * xf, axis=-1, keepdims=True) + EPS) * g.astype(jnp.float32)
    return y.astype(x.dtype)


def swiglu(h, wg, wu, wd):
    return (jax.nn.silu(h @ wg) * (h @ wu)) @ wd


def modulation(c, w, b):
    m = jax.nn.silu(c.astype(jnp.float32)) @ w.astype(jnp.float32) + b.astype(jnp.float32)
    return jnp.split(m.astype(c.dtype)[:, None, :], 6, axis=-1)


def mixer_inputs(x, shift, scale, g_pre, w_in, w_gla_a2, b_gla_a):
    bsz, t, _ = x.shape
    h = rmsnorm(x, g_pre) * (1 + scale) + shift
    p = h @ w_in
    base = 2 * GLA_QK_W + 2 * GLA_V_W + GLA_GATE_RANK
    cuts = [GLA_QK_W, 2 * GLA_QK_W, 2 * GLA_QK_W + GLA_V_W, 2 * GLA_QK_W + 2 * GLA_V_W, base, base + SB_W, base + 2 * SB_W]
    qa, ka, va, ga, ra, qb, kb, vb = jnp.split(p, cuts, axis=-1)
    log_a = jax.nn.log_sigmoid((ra @ w_gla_a2 + b_gla_a).astype(jnp.float32)) / GLA_TAU
    return (qa.reshape(bsz, t, GLA_HEADS, GLA_DK) * GLA_DK ** -0.5,
            ka.reshape(bsz, t, GLA_HEADS, GLA_DK),
            va.reshape(bsz, t, GLA_HEADS, GLA_DV),
            ga,
            log_a.reshape(bsz, t, GLA_HEADS, GLA_DK),
            qb.reshape(bsz, t, SB_HEADS, SB_DH),
            kb.reshape(bsz, t, SB_HEADS, SB_DH),
            vb.reshape(bsz, t, SB_HEADS, SB_DH))


def gla_chunk(state, q, k, v, log_a):
    n = q.shape[2]
    b = jnp.cumsum(log_a, axis=2)
    o_inter = jnp.einsum('bhtk,bhkv->bhtv', q * jnp.exp(b), state)
    causal = jnp.tril(jnp.ones((n, n), dtype=bool))
    decay = jnp.exp(jnp.where(causal[:, :, None], b[:, :, :, None, :] - b[:, :, None, :, :], -jnp.inf))
    scores = jnp.einsum('bhtk,bhtsk,bhsk->bhts', q, decay, k)
    o = o_inter + jnp.einsum('bhts,bhsv->bhtv', scores, v)
    b_last = b[:, :, -1, :]
    new_state = jnp.exp(b_last)[..., None] * state + jnp.einsum('bhsk,bhsv->bhkv', k * jnp.exp(b_last[:, :, None, :] - b), v)
    return o, new_state


def gla_prompt(q, k, v, log_a):
    bsz, t = q.shape[:2]
    nc = t // GLA_CHUNK

    def chunks(a):
        return a.astype(jnp.float32).reshape(bsz, nc, GLA_CHUNK, GLA_HEADS, -1).transpose(1, 0, 3, 2, 4)

    def step(state, xs):
        o, state = gla_chunk(state, *xs)
        return state, o

    s0 = jnp.zeros((bsz, GLA_HEADS, GLA_DK, GLA_DV), jnp.float32)
    s_final, o = lax.scan(step, s0, (chunks(q), chunks(k), chunks(v), chunks(log_a)))
    return o.transpose(1, 0, 3, 2, 4).reshape(bsz, t, GLA_HEADS, GLA_DV), s_final


def gla_continue(q, k, v, log_a, state):
    def heads_first(a):
        return a.astype(jnp.float32).transpose(0, 2, 1, 3)

    o, new_state = gla_chunk(state.astype(jnp.float32), heads_first(q), heads_first(k), heads_first(v), heads_first(log_a))
    return o.transpose(0, 2, 1, 3), new_state


def sb_attend(q, k, v, q_pos, k_pos, bias):
    z = jnp.einsum('...hqd,...hkd->...hqk', q.astype(jnp.float32), k.astype(jnp.float32)) * SB_DH ** -0.5
    z = z + bias.astype(jnp.float32)[:, None, None]
    causal = k_pos[None, :] < q_pos[:, None]
    log_keep = jnp.where(causal, jax.nn.log_sigmoid(-z), 0.0)
    later = lax.cumsum(log_keep, axis=z.ndim - 1, reverse=True) - log_keep
    w = jnp.where(causal, jnp.exp(jax.nn.log_sigmoid(z) + later), 0.0)
    return jnp.einsum('...hqk,...hkd->...hqd', w, v.astype(jnp.float32))


def sb_prompt(q, k, v, bias):
    bsz, t = q.shape[:2]
    nb = t // SB_BLOCK
    kh = k.transpose(0, 2, 1, 3)
    vh = v.transpose(0, 2, 1, 3)
    qb = q.reshape(bsz, nb, SB_BLOCK, SB_HEADS, SB_DH).transpose(1, 0, 3, 2, 4)
    k_pos = jnp.arange(t)

    def block(args):
        qi, i = args
        return sb_attend(qi, kh, vh, i * SB_BLOCK + jnp.arange(SB_BLOCK), k_pos, bias)

    o = lax.map(block, (qb, jnp.arange(nb)))
    return o.transpose(1, 0, 3, 2, 4).reshape(bsz, t, SB_HEADS, SB_DH)


def sb_sample(q, k, v, cache_k, cache_v, page_table, bias):
    t = q.shape[1]
    past = page_table.shape[1] * cache_k.shape[1]
    q_pos = past + jnp.arange(t)
    k_pos = jnp.arange(past + t)

    def one(args):
        qi, ki, vi, pages = args
        kk = jnp.concatenate([cache_k[pages].reshape(past, SB_HEADS, SB_DH), ki.astype(cache_k.dtype)], 0)
        vv = jnp.concatenate([cache_v[pages].reshape(past, SB_HEADS, SB_DH), vi.astype(cache_v.dtype)], 0)
        return sb_attend(qi.transpose(1, 0, 2), kk.transpose(1, 0, 2), vv.transpose(1, 0, 2), q_pos, k_pos, bias)

    o = lax.map(one, (q, k, v, page_table))
    return o.transpose(0, 2, 1, 3)


def mixer_output(oa, ga, ob, g_norm, w_out, dtype):
    bsz, t = oa.shape[:2]
    a = rmsnorm(oa, g_norm) * jax.nn.silu(ga.astype(jnp.float32)).reshape(bsz, t, GLA_HEADS, GLA_DV)
    merged = jnp.concatenate([a.reshape(bsz, t, GLA_V_W), ob.reshape(bsz, t, SB_W)], -1).astype(dtype)
    return merged @ w_out


def moe(h, w_router, b_router, w_e_gate, w_e_up, w_e_down):
    n, d = h.shape
    scores = jax.nn.sigmoid(h.astype(jnp.float32) @ w_router.astype(jnp.float32))
    biased = scores + b_router.astype(jnp.float32)
    grp = biased.reshape(n, N_GROUPS, N_EXPERTS // N_GROUPS)
    grp_score = lax.top_k(grp, 2)[0].sum(-1)
    _, top_grp = lax.top_k(grp_score, TOPK_GROUPS)
    grp_mask = jax.nn.one_hot(top_grp, N_GROUPS, dtype=jnp.float32).sum(1) > 0
    exp_mask = jnp.repeat(grp_mask, N_EXPERTS // N_GROUPS, axis=1)
    _, top_e = lax.top_k(jnp.where(exp_mask, biased, -jnp.inf), TOP_K)
    gate = jnp.take_along_axis(scores, top_e, axis=1)
    gate = gate / gate.sum(-1, keepdims=True) * ROUTED_SCALE
    a = n * TOP_K
    flat_e = top_e.reshape(a)
    flat_tok = jnp.repeat(jnp.arange(n, dtype=jnp.int32), TOP_K)
    flat_gate = gate.reshape(a)
    order = jnp.argsort(flat_e)
    e_s = flat_e[order]
    counts = jnp.bincount(flat_e, length=N_EXPERTS)
    padded = (counts + MOE_BLOCK - 1) // MOE_BLOCK * MOE_BLOCK
    pad_end = jnp.cumsum(padded)
    pad_start = pad_end - padded
    start = jnp.cumsum(counts) - counts
    dest = pad_start[e_s] + jnp.arange(a) - start[e_s]
    n_blocks = -(-a // MOE_BLOCK) + N_EXPERTS
    rows = n_blocks * MOE_BLOCK
    row_tok = jnp.full((rows,), n, jnp.int32).at[dest].set(flat_tok[order])
    row_gate = jnp.zeros((rows,), jnp.float32).at[dest].set(flat_gate[order])
    blk_e = jnp.minimum(jnp.searchsorted(pad_end, jnp.arange(n_blocks) * MOE_BLOCK, side='right'), N_EXPERTS - 1)
    h_ext = jnp.concatenate([h, jnp.zeros((1, d), h.dtype)], 0)

    def run(args):
        tok, g, e = args
        y = swiglu(h_ext[tok], w_e_gate[e], w_e_up[e], w_e_down[e])
        return y.astype(jnp.float32) * g[:, None]

    yb = lax.map(run, (row_tok.reshape(n_blocks, MOE_BLOCK), row_gate.reshape(n_blocks, MOE_BLOCK), blk_e))
    out = jax.ops.segment_sum(yb.reshape(rows, d), row_tok, num_segments=n + 1)[:n]
    return out.astype(h.dtype)


def channel_sublayer(xp, xs, mod_p, mod_s, g_pre, g_post, w_router, b_router, w_e_gate, w_e_up, w_e_down, w_s_gate, w_s_up, w_s_down):
    shift_p, scale_p, gate_p = mod_p
    shift_s, scale_s, gate_s = mod_s
    hp = (rmsnorm(xp, g_pre) * (1 + scale_p) + shift_p).reshape(-1, D_MODEL)
    hs = (rmsnorm(xs, g_pre) * (1 + scale_s) + shift_s).reshape(-1, D_MODEL)
    h = jnp.concatenate([hp, hs], 0)
    f = rmsnorm(moe(h, w_router, b_router, w_e_gate, w_e_up, w_e_down) + swiglu(h, w_s_gate, w_s_up, w_s_down), g_post)
    n_p = hp.shape[0]
    xp = xp + gate_p * f[:n_p].reshape(xp.shape)
    xs = xs + gate_s * f[n_p:].reshape(xs.shape)
    return xp, xs


def setup_inputs(seed: int = 0) -> dict:
    key = jax.random.key(seed)
    ks = jax.random.split(key, 32)
    n_pages = PAST_LEN // PAGE_SIZE
    in_use = DEC_BATCH * n_pages
    n_pool = in_use + max(1, in_use // 4)
    L = DEPTH

    def nrm(k, shape, s=1.0):
        return jax.random.normal(k, shape, jnp.float32) * s

    page_table = jax.random.permutation(ks[7], n_pool)[:in_use].reshape(DEC_BATCH, n_pages).astype(jnp.int32)
    return {
        'x_prompt': nrm(ks[0], (BATCH, SEQ, D_MODEL)),
        'x_sample': nrm(ks[1], (DEC_BATCH, DEC_SEQ, D_MODEL)),
        'c_prompt': nrm(ks[2], (BATCH, D_MODEL)),
        'c_sample': nrm(ks[3], (DEC_BATCH, D_MODEL)),
        'cache_sb_k': nrm(ks[4], (L, n_pool, PAGE_SIZE, SB_HEADS, SB_DH)),
        'cache_sb_v': nrm(ks[5], (L, n_pool, PAGE_SIZE, SB_HEADS, SB_DH)),
        'state_gla': nrm(ks[6], (L, DEC_BATCH, GLA_HEADS, GLA_DK, GLA_DV), 0.5),
        'page_table': page_table,
        'w_ada': nrm(ks[8], (L, D_MODEL, 6 * D_MODEL), 0.5 * D_MODEL ** -0.5),
        'b_ada': nrm(ks[9], (L, 6 * D_MODEL), 0.02),
        'g_pre_mix': 1.0 + nrm(ks[10], (L, D_MODEL), 0.02),
        'g_post_mix': 1.0 + nrm(ks[11], (L, D_MODEL), 0.02),
        'g_pre_ffn': 1.0 + nrm(ks[12], (L, D_MODEL), 0.02),
        'g_post_ffn': 1.0 + nrm(ks[13], (L, D_MODEL), 0.02),
        'w_in': nrm(ks[14], (L, D_MODEL, IN_W), D_MODEL ** -0.5),
        'w_gla_a2': nrm(ks[15], (L, GLA_GATE_RANK, GLA_QK_W), GLA_GATE_RANK ** -0.5),
        'b_gla_a': nrm(ks[16], (L, GLA_QK_W), 0.1),
        'g_gla_norm': 1.0 + nrm(ks[17], (L, GLA_DV), 0.02),
        'b_sb': SB_BIAS_INIT + nrm(ks[27], (L, SB_HEADS), 0.1),
        'w_out': nrm(ks[18], (L, MIX_W, D_MODEL), MIX_W ** -0.5),
        'w_router': nrm(ks[19], (L, D_MODEL, N_EXPERTS), D_MODEL ** -0.5),
        'b_router': nrm(ks[20], (L, N_EXPERTS), 0.01),
        'w_e_gate': nrm(ks[21], (L, N_EXPERTS, D_MODEL, D_EXPERT), D_MODEL ** -0.5),
        'w_e_up': nrm(ks[22], (L, N_EXPERTS, D_MODEL, D_EXPERT), D_MODEL ** -0.5),
        'w_e_down': nrm(ks[23], (L, N_EXPERTS, D_EXPERT, D_MODEL), D_EXPERT ** -0.5),
        'w_s_gate': nrm(ks[24], (L, D_MODEL, D_SHARED), D_MODEL ** -0.5),
        'w_s_up': nrm(ks[25], (L, D_MODEL, D_SHARED), D_MODEL ** -0.5),
        'w_s_down': nrm(ks[26], (L, D_SHARED, D_MODEL), D_SHARED ** -0.5),
    }


def reference(x_prompt, x_sample, c_prompt, c_sample, cache_sb_k, cache_sb_v, state_gla, page_table,
              w_ada, b_ada, g_pre_mix, g_post_mix, g_pre_ffn, g_post_ffn, w_in, w_gla_a2, b_gla_a,
              g_gla_norm, b_sb, w_out, w_router, b_router, w_e_gate, w_e_up, w_e_down, w_s_gate, w_s_up, w_s_down):
    xp, xs = x_prompt, x_sample
    k_p, v_p, s_p, k_s, v_s, s_s = [], [], [], [], [], []
    for l in range(DEPTH):
        mod_p = modulation(c_prompt, w_ada[l], b_ada[l])
        mod_s = modulation(c_sample, w_ada[l], b_ada[l])
        qa, ka, va, ga, la, qb, kb, vb = mixer_inputs(xp, mod_p[0], mod_p[1], g_pre_mix[l], w_in[l], w_gla_a2[l], b_gla_a[l])
        oa, st = gla_prompt(qa, ka, va, la)
        ob = sb_prompt(qb, kb, vb, b_sb[l])
        xp = xp + mod_p[2] * rmsnorm(mixer_output(oa, ga, ob, g_gla_norm[l], w_out[l], xp.dtype), g_post_mix[l])
        k_p.append(kb.astype(cache_sb_k.dtype))
        v_p.append(vb.astype(cache_sb_v.dtype))
        s_p.append(st.astype(state_gla.dtype))
        qa, ka, va, ga, la, qb, kb, vb = mixer_inputs(xs, mod_s[0], mod_s[1], g_pre_mix[l], w_in[l], w_gla_a2[l], b_gla_a[l])
        oa, st = gla_continue(qa, ka, va, la, state_gla[l])
        ob = sb_sample(qb, kb, vb, cache_sb_k[l], cache_sb_v[l], page_table, b_sb[l])
        xs = xs + mod_s[2] * rmsnorm(mixer_output(oa, ga, ob, g_gla_norm[l], w_out[l], xs.dtype), g_post_mix[l])
        k_s.append(kb.astype(cache_sb_k.dtype))
        v_s.append(vb.astype(cache_sb_v.dtype))
        s_s.append(st.astype(state_gla.dtype))
        xp, xs = channel_sublayer(xp, xs, mod_p[3:], mod_s[3:], g_pre_ffn[l], g_post_ffn[l], w_router[l], b_router[l],
                                  w_e_gate[l], w_e_up[l], w_e_down[l], w_s_gate[l], w_s_up[l], w_s_down[l])
    return (xp, xs, jnp.stack(k_p), jnp.stack(v_p), jnp.stack(s_p), jnp.stack(k_s), jnp.stack(v_s), jnp.stack(s_s))
```

```python
import functools
import math

import jax
import jax.numpy as jnp
from jax import lax
from jax.experimental import pallas as pl
from jax.experimental.pallas import tpu as pltpu

EPS = 1e-6
GLA_TAU = 16.0
TOP_K = 8
N_GROUPS = 8
TOPK_GROUPS = 4
ROUTED_SCALE = 2.5

LANES = 128
SUBLANES = 8
VMEM_LIMIT = 56 * 1024 * 1024

GLA_CHUNK = 64
SB_TQ = 256
SB_TK = 256
SB_PAGES_PER_STEP = 8
MOE_TM = 256
NEW_POS_PAD = 16

bf16 = jnp.bfloat16
f32 = jnp.float32


def _dot(a, b):
    return jnp.dot(a, b, preferred_element_type=f32)


def _dot_nt(a, b):
    return lax.dot_general(a, b, (((1,), (1,)), ((), ())), preferred_element_type=f32)


def _dot_tn(a, b):
    return lax.dot_general(a, b, (((0,), (0,)), ((), ())), preferred_element_type=f32)


def _softplus(z):
    return jnp.maximum(z, 0.0) + jnp.log(1.0 + jnp.exp(-jnp.abs(z)))


def _silu(x):
    return x * (1.0 / (1.0 + jnp.exp(-x)))


def _params(sem):
    return pltpu.CompilerParams(dimension_semantics=sem, vmem_limit_bytes=VMEM_LIMIT)


def _pick_tile(n, candidates):
    for c in candidates:
        if n % c == 0:
            return c
    raise ValueError(f"no tile in {candidates} divides {n}")


def _modulation_kernel(c_ref, w_ref, b_ref, o_ref):
    a = _silu(c_ref[...]).astype(bf16)
    o_ref[...] = _dot(a, w_ref[...].astype(bf16)) + b_ref[...]


def _modulation(c_all, w_ada, b_ada):
    m, d = c_all.shape
    n = w_ada.shape[1]
    tn = _pick_tile(n, (512, 256, 128))
    return pl.pallas_call(
        _modulation_kernel,
        out_shape=jax.ShapeDtypeStruct((m, n), f32),
        grid=(n // tn,),
        in_specs=[
            pl.BlockSpec((m, d), lambda j: (0, 0)),
            pl.BlockSpec((d, tn), lambda j: (0, j)),
            pl.BlockSpec((1, tn), lambda j: (0, j)),
        ],
        out_specs=pl.BlockSpec((m, tn), lambda j: (0, j)),
        compiler_params=_params(("arbitrary",)),
        name="modulation",
    )(c_all, w_ada, b_ada.reshape(1, n))


def _inproj_kernel(x_ref, shift_ref, scale_ref, g_ref, w_ref, wra_ref, wa2_ref, ba_ref,
                   p_ref, k_ref, v_ref, h_scr, *, n_p, n_k, d_model):
    j = pl.program_id(1)
    n_w = n_p + 2 * n_k

    @pl.when(j == 0)
    def _():
        x = x_ref[...]
        r = lax.rsqrt(jnp.sum(x * x, axis=-1, keepdims=True) * (1.0 / d_model) + EPS)
        h = x * r * g_ref[...]
        h = h * (1.0 + scale_ref[0]) + shift_ref[0]
        h_scr[...] = h.astype(bf16)

    @pl.when(j < n_p)
    def _():
        p_ref[...] = _dot(h_scr[...], w_ref[...])

    @pl.when(jnp.logical_and(j >= n_p, j < n_p + n_k))
    def _():
        k_ref[...] = _dot(h_scr[...], w_ref[...])

    @pl.when(jnp.logical_and(j >= n_p + n_k, j < n_w))
    def _():
        v_ref[...] = _dot(h_scr[...], w_ref[...])

    @pl.when(j >= n_w)
    def _():
        ra = _dot(h_scr[...], wra_ref[...]).astype(bf16)
        xa = _dot(ra, wa2_ref[...]) + ba_ref[...]
        log_sig = jnp.minimum(xa, 0.0) - jnp.log(1.0 + jnp.exp(-jnp.abs(xa)))
        p_ref[...] = log_sig * (1.0 / GLA_TAU)


def _inproj(x2d, shift, scale, rows_per_mod, g_pre, w_main, w_ra, w_a2, b_a, *, tm, tn, qk_w, v_w, sb_w):
    m, d = x2d.shape
    n_p = (2 * qk_w + 2 * v_w + sb_w) // tn
    n_k = sb_w // tn
    n_l = qk_w // tn
    n_w = n_p + 2 * n_k
    assert w_main.shape[1] == n_w * tn
    mod_rows = shift.shape[1]
    tiles_per_mod = rows_per_mod // tm if mod_rows == 1 else 1

    def mod_map(i, j):
        return (i // tiles_per_mod, 0, 0)

    def p_map(i, j):
        return (i, jnp.where(j < n_p, j, jnp.where(j < n_w, n_p - 1, n_p + j - n_w)))

    kernel = functools.partial(_inproj_kernel, n_p=n_p, n_k=n_k, d_model=d)
    return pl.pallas_call(
        kernel,
        out_shape=(
            jax.ShapeDtypeStruct((m, (n_p + n_l) * tn), f32),
            jax.ShapeDtypeStruct((m, sb_w), f32),
            jax.ShapeDtypeStruct((m, sb_w), f32),
        ),
        grid=(m // tm, n_w + n_l),
        in_specs=[
            pl.BlockSpec((tm, d), lambda i, j: (i, 0)),
            pl.BlockSpec((1, mod_rows, d), mod_map),
            pl.BlockSpec((1, mod_rows, d), mod_map),
            pl.BlockSpec((1, d), lambda i, j: (0, 0)),
            pl.BlockSpec((d, tn), lambda i, j: (0, jnp.minimum(j, n_w - 1))),
            pl.BlockSpec((d, LANES), lambda i, j: (0, 0)),
            pl.BlockSpec((LANES, tn), lambda i, j: (0, jnp.clip(j - n_w, 0, n_l - 1))),
            pl.BlockSpec((1, tn), lambda i, j: (0, jnp.clip(j - n_w, 0, n_l - 1))),
        ],
        out_specs=(
            pl.BlockSpec((tm, tn), p_map),
            pl.BlockSpec((tm, tn), lambda i, j: (i, jnp.clip(j - n_p, 0, n_k - 1))),
            pl.BlockSpec((tm, tn), lambda i, j: (i, jnp.clip(j - n_p - n_k, 0, n_k - 1))),
        ),
        scratch_shapes=[pltpu.VMEM((tm, d), bf16)],
        compiler_params=_params(("arbitrary", "arbitrary")),
        name="inproj",
    )(x2d, shift, scale, g_pre, w_main, w_ra, w_a2, b_a)


def _gla_kernel(q_ref, k_ref, v_ref, la_ref, s0_ref, o_ref, s_ref, *, heads, dk, dv, chunk, n_valid):
    c = pl.program_id(1)

    @pl.when(c == 0)
    def _():
        s_ref[...] = s0_ref[...]

    row = lax.broadcasted_iota(jnp.int32, (chunk, chunk), 0)
    col = lax.broadcasted_iota(jnp.int32, (chunk, chunk), 1)
    tril = col <= row
    tri = jnp.where(tril, 1.0, 0.0).astype(bf16)
    ones = jnp.ones((chunk, LANES), bf16)
    mid = chunk // 2
    if n_valid < chunk:
        valid = lax.broadcasted_iota(jnp.int32, (chunk, 1), 0) < n_valid

    for h in range(heads):
        q = q_ref[:, h * dk:(h + 1) * dk] * (dk ** -0.5)
        k = k_ref[:, h * dk:(h + 1) * dk]
        v = v_ref[:, h * dv:(h + 1) * dv]
        la = la_ref[:, h * dk:(h + 1) * dk]
        if n_valid < chunk:
            la = jnp.where(valid, la, 0.0)
            k = jnp.where(valid, k, 0.0)
            v = jnp.where(valid, v, 0.0)
        la_hi = la.astype(bf16)
        la_lo = (la - la_hi.astype(f32)).astype(bf16)
        b = _dot(tri, la_hi) + _dot(tri, la_lo)
        b_last_col = _dot_tn(la_hi, ones) + _dot_tn(la_lo, ones)
        b_ref = b[mid:mid + 1, :]
        b_last = b[chunk - 1:chunk, :]
        s_prev = s_ref[0, h]
        qe = (q * jnp.exp(b - b_ref)).astype(bf16)
        ke = (k * jnp.exp(b_ref - b)).astype(bf16)
        scores = jnp.where(tril, _dot_nt(qe, ke), 0.0).astype(bf16)
        qi = (q * jnp.exp(b)).astype(bf16)
        o = _dot(qi, s_prev.astype(bf16)) + _dot(scores, v.astype(bf16))
        o_ref[:, h * dv:(h + 1) * dv] = o
        kd = (k * jnp.exp(b_last - b)).astype(bf16)
        decay = jnp.exp(b_last_col)
        decay = jnp.concatenate([decay] * (dv // LANES), axis=1)
        s_ref[0, h] = decay * s_prev + _dot_tn(kd, v.astype(bf16))


def _gla(q_arr, k_arr, v_arr, la_arr, col_blocks, s0, *, batch, n_chunks, chunk, n_valid, heads, dk, dv):
    qc, kc, vc, lc = col_blocks
    m = batch * n_chunks * chunk
    qk_w, v_w = heads * dk, heads * dv
    kernel = functools.partial(_gla_kernel, heads=heads, dk=dk, dv=dv, chunk=chunk, n_valid=n_valid)

    def rows(b, c):
        return b * n_chunks + c

    return pl.pallas_call(
        kernel,
        out_shape=(
            jax.ShapeDtypeStruct((m, v_w), f32),
            jax.ShapeDtypeStruct((batch, heads, dk, dv), f32),
        ),
        grid=(batch, n_chunks),
        in_specs=[
            pl.BlockSpec((chunk, qk_w), lambda b, c: (rows(b, c), qc)),
            pl.BlockSpec((chunk, qk_w), lambda b, c: (rows(b, c), kc)),
            pl.BlockSpec((chunk, v_w), lambda b, c: (rows(b, c), vc)),
            pl.BlockSpec((chunk, qk_w), lambda b, c: (rows(b, c), lc)),
            pl.BlockSpec((1, heads, dk, dv), lambda b, c: (b, 0, 0, 0)),
        ],
        out_specs=(
            pl.BlockSpec((chunk, v_w), lambda b, c: (rows(b, c), 0)),
            pl.BlockSpec((1, heads, dk, dv), lambda b, c: (b, 0, 0, 0)),
        ),
        compiler_params=_params(("arbitrary", "arbitrary")),
        name="gla",
    )(q_arr, k_arr, v_arr, la_arr, s0)


def _sb_prompt_kernel(bias_ref, q_ref, k_ref, v_ref, u_ref, o_ref, k_scr, v_scr, *, tq, tk, dh):
    h = pl.program_id(1)
    i = pl.program_id(2)

    @pl.when(i == 0)
    def _():
        k_scr[...] = k_ref[...].astype(bf16)
        v_scr[...] = v_ref[...].astype(bf16)

    q = (q_ref[...] * (dh ** -0.5)).astype(bf16)
    bias = bias_ref[h]
    u = u_ref[...]
    n_sub = tq // tk

    def block(j, carry, acc, q_rows, diagonal):
        kj = k_scr[pl.ds(pl.multiple_of(j * tk, tk), tk), :]
        vj = v_scr[pl.ds(pl.multiple_of(j * tk, tk), tk), :]
        z = _dot_nt(q_rows, kj) + bias
        sp = _softplus(z)
        log_w = z - sp
        if diagonal:
            causal = (lax.broadcasted_iota(jnp.int32, z.shape, 1)
                      < lax.broadcasted_iota(jnp.int32, z.shape, 0))
            sp = jnp.where(causal, sp, 0.0)
        later = _dot(sp.astype(bf16), u) + carry
        w = jnp.exp(log_w - later)
        if diagonal:
            w = jnp.where(causal, w, 0.0)
        acc = acc + _dot(w.astype(bf16), vj)
        carry = carry + jnp.sum(sp, axis=-1, keepdims=True)
        return carry, acc

    assert n_sub == 1
    carry = jnp.zeros((tq, 1), f32)
    acc = jnp.zeros((tq, dh), f32)
    carry, acc = block(i, carry, acc, q, True)

    def body(s, state):
        return block(i - 1 - s, state[0], state[1], q, False)

    carry, acc = lax.fori_loop(0, i, body, (carry, acc))
    o_ref[...] = acc


def _sb_prompt(p_arr, q_col0, k_arr, v_arr, bias, *, batch, seq, heads, dh):
    tq, tk = SB_TQ, SB_TK
    nq = seq // tq
    m = batch * seq
    u = (jnp.arange(tk)[:, None] > jnp.arange(tk)[None, :]).astype(bf16)
    kernel = functools.partial(_sb_prompt_kernel, tq=tq, tk=tk, dh=dh)
    return pl.pallas_call(
        kernel,
        out_shape=jax.ShapeDtypeStruct((m, heads * dh), f32),
        grid_spec=pltpu.PrefetchScalarGridSpec(
            num_scalar_prefetch=1,
            grid=(batch, heads, nq),
            in_specs=[
                pl.BlockSpec((tq, dh), lambda b, h, i, s: (b * nq + i, q_col0 + h)),
                pl.BlockSpec((seq, dh), lambda b, h, i, s: (b, h)),
                pl.BlockSpec((seq, dh), lambda b, h, i, s: (b, h)),
                pl.BlockSpec((tk, tk), lambda b, h, i, s: (0, 0)),
            ],
            out_specs=pl.BlockSpec((tq, dh), lambda b, h, i, s: (b * nq + i, h)),
            scratch_shapes=[pltpu.VMEM((seq, dh), bf16), pltpu.VMEM((seq, dh), bf16)],
        ),
        compiler_params=_params(("arbitrary", "arbitrary", "arbitrary")),
        name="sb_prompt",
    )(bias, p_arr, k_arr, v_arr, u)


def _sb_sample_kernel(pt_ref, q_ref, bias_ref, knew_ref, vnew_ref, u_ref, *rest,
                      pages_per_step, heads, n_q, dh, page):
    k_refs = rest[:pages_per_step]
    v_refs = rest[pages_per_step:2 * pages_per_step]
    o_ref, carry_scr = rest[2 * pages_per_step:]
    s = pl.program_id(1)
    n_rows = heads * n_q
    q = (q_ref[0] * (dh ** -0.5)).astype(bf16)
    bias = bias_ref[...][:, :1]
    u = u_ref[...]

    def attend(k2, v2, mask, carry):
        n_cols = k2.shape[0]
        n_blk = n_cols // LANES
        z = _dot_nt(q, k2) + bias
        sp_full = _softplus(z)
        log_w = z - sp_full
        sp = jnp.where(mask, sp_full, 0.0).astype(bf16)
        stacked = jnp.concatenate([sp[:, c * LANES:(c + 1) * LANES] for c in range(n_blk)], axis=0)
        cum_tot = _dot(stacked, u)
        ws = [None] * n_blk
        for c in range(n_blk - 1, -1, -1):
            ct = cum_tot[c * n_rows:(c + 1) * n_rows]
            later = ct[:, :LANES] + carry
            w = jnp.exp(log_w[:, c * LANES:(c + 1) * LANES] - later)
            ws[c] = jnp.where(mask[:, c * LANES:(c + 1) * LANES], w, 0.0).astype(bf16)
            carry = carry + ct[:, LANES:]
        return _dot(jnp.concatenate(ws, axis=1), v2), carry

    row_head = lax.broadcasted_iota(jnp.int32, (n_rows, 1), 0) // n_q
    row_q = lax.broadcasted_iota(jnp.int32, (n_rows, 1), 0) % n_q

    @pl.when(s == 0)
    def _():
        n_cols = knew_ref.shape[1]
        lane = lax.broadcasted_iota(jnp.int32, (n_rows, n_cols), 1)
        mask = jnp.logical_and(lane % heads == row_head, lane // heads < row_q)
        out, carry = attend(knew_ref[0].astype(bf16), vnew_ref[0].astype(bf16), mask,
                            jnp.zeros((n_rows, LANES), f32))
        o_ref[0] = out
        carry_scr[...] = carry

    n_cols = page * heads
    lane = lax.broadcasted_iota(jnp.int32, (n_rows, n_cols), 1)
    mask = lane % heads == row_head
    carry = carry_scr[...]
    acc = o_ref[0]
    for r in range(pages_per_step):
        k2 = k_refs[r][0, 0].reshape(n_cols, dh).astype(bf16)
        v2 = v_refs[r][0, 0].reshape(n_cols, dh).astype(bf16)
        out, carry = attend(k2, v2, mask, carry)
        acc = acc + out
    o_ref[0] = acc
    carry_scr[...] = carry


def _sb_sample(q_rows, bias_rows, k_new, v_new, cache_k, cache_v, layer, page_table, *, heads, n_q, dh):
    db, n_pages = page_table.shape
    page = cache_k.shape[2]
    pps = SB_PAGES_PER_STEP
    assert n_pages % pps == 0 and heads == SUBLANES and dh == LANES
    n_steps = n_pages // pps
    n_rows = heads * n_q
    lane = jnp.arange(LANES)
    u = jnp.concatenate([(lane[:, None] > lane[None, :]).astype(bf16), jnp.ones((LANES, LANES), bf16)], axis=1)

    def page_map(r):
        def index(b, s, pt):
            return (layer, pt[b, (n_steps - 1 - s) * pps + (pps - 1 - r)], 0, 0, 0)
        return index

    cache_spec = [pl.BlockSpec((1, 1, page, heads, dh), page_map(r)) for r in range(pps)]
    kernel = functools.partial(_sb_sample_kernel, pages_per_step=pps, heads=heads, n_q=n_q, dh=dh, page=page)
    return pl.pallas_call(
        kernel,
        out_shape=jax.ShapeDtypeStruct((db, n_rows, dh), f32),
        grid_spec=pltpu.PrefetchScalarGridSpec(
            num_scalar_prefetch=1,
            grid=(db, n_steps),
            in_specs=[
                pl.BlockSpec((1, n_rows, dh), lambda b, s, pt: (b, 0, 0)),
                pl.BlockSpec((n_rows, LANES), lambda b, s, pt: (0, 0)),
                pl.BlockSpec((1, NEW_POS_PAD * heads, dh), lambda b, s, pt: (b, 0, 0)),
                pl.BlockSpec((1, NEW_POS_PAD * heads, dh), lambda b, s, pt: (b, 0, 0)),
                pl.BlockSpec((LANES, 2 * LANES), lambda b, s, pt: (0, 0)),
            ] + cache_spec + cache_spec,
            out_specs=pl.BlockSpec((1, n_rows, dh), lambda b, s, pt: (b, 0, 0)),
            scratch_shapes=[pltpu.VMEM((n_rows, LANES), f32)],
        ),
        compiler_params=_params(("arbitrary", "arbitrary")),
        name="sb_sample",
    )(page_table, q_rows, bias_rows, k_new, v_new, u, *([cache_k] * pps), *([cache_v] * pps))


def _mixout_kernel(oa_ref, ga_ref, ob_ref, x_ref, gate1_ref, shift2_ref, scale2_ref, ggla_ref, wout_ref,
                   gpost_ref, gpre_ref, wr_hi_ref, wr_lo_ref, x1_ref, h2_ref, lg_ref,
                   *, heads, dv, d_model, n_main):
    @pl.when(pl.program_id(0) >= n_main)
    def _():
        h2_ref[...] = jnp.zeros_like(h2_ref)
        lg_ref[...] = jnp.zeros_like(lg_ref)

    @pl.when(pl.program_id(0) < n_main)
    def _():
        _mixout_tile(oa_ref, ga_ref, ob_ref, x_ref, gate1_ref, shift2_ref, scale2_ref, ggla_ref, wout_ref,
                     gpost_ref, gpre_ref, wr_hi_ref, wr_lo_ref, x1_ref, h2_ref, lg_ref,
                     heads=heads, dv=dv, d_model=d_model)


def _mixout_tile(oa_ref, ga_ref, ob_ref, x_ref, gate1_ref, shift2_ref, scale2_ref, ggla_ref, wout_ref,
                 gpost_ref, gpre_ref, wr_hi_ref, wr_lo_ref, x1_ref, h2_ref, lg_ref, *, heads, dv, d_model):
    parts = []
    ggla = ggla_ref[...]
    for h in range(heads):
        oa = oa_ref[:, h * dv:(h + 1) * dv]
        r = lax.rsqrt(jnp.sum(oa * oa, axis=-1, keepdims=True) * (1.0 / dv) + EPS)
        ga = ga_ref[:, h * dv:(h + 1) * dv]
        parts.append((oa * r * ggla * _silu(ga)).astype(bf16))
    parts.append(ob_ref[...].astype(bf16))
    merged = jnp.concatenate(parts, axis=1)
    o = _dot(merged, wout_ref[...])
    r = lax.rsqrt(jnp.sum(o * o, axis=-1, keepdims=True) * (1.0 / d_model) + EPS)
    x1 = x_ref[...] + gate1_ref[0] * (o * r * gpost_ref[...])
    x1_ref[...] = x1
    r = lax.rsqrt(jnp.sum(x1 * x1, axis=-1, keepdims=True) * (1.0 / d_model) + EPS)
    h2 = x1 * r * gpre_ref[...] * (1.0 + scale2_ref[0]) + shift2_ref[0]
    h2_ref[...] = h2
    h_hi = h2.astype(bf16)
    h_lo = (h2 - h_hi.astype(f32)).astype(bf16)
    lg_ref[...] = (_dot_nt(wr_hi_ref[...], h_hi) + _dot_nt(wr_lo_ref[...], h_hi)
                   + _dot_nt(wr_hi_ref[...], h_lo))


def _mixout(oa, p_arr, ga_col, ob, x2d, gate1, shift2, scale2, rows_per_mod, g_gla, w_out, g_post, g_pre_ffn,
            wr_hi, wr_lo, shared, n_all, row_off, *, tm, heads, dv):
    m, d = x2d.shape
    v_w = heads * dv
    sb_w = ob.shape[1]
    n_e = wr_hi.shape[0]
    mod_rows = gate1.shape[1]
    tiles_per_mod = rows_per_mod // tm if mod_rows == 1 else 1
    off = row_off // tm
    assert row_off % tm == 0

    n_main = m // tm
    tail = n_all - (row_off + m) if shared is None else 0
    assert 0 <= tail <= tm
    n_steps = n_main + (1 if tail else 0)

    def row(i):
        return jnp.minimum(i, n_main - 1)

    def mod_map(i):
        return (row(i) // tiles_per_mod, 0, 0)

    kernel = functools.partial(_mixout_kernel, heads=heads, dv=dv, d_model=d, n_main=n_main)
    in_specs = [
        pl.BlockSpec((tm, v_w), lambda i: (row(i), 0)),
        pl.BlockSpec((tm, v_w), lambda i: (row(i), ga_col)),
        pl.BlockSpec((tm, sb_w), lambda i: (row(i), 0)),
        pl.BlockSpec((tm, d), lambda i: (row(i), 0)),
        pl.BlockSpec((1, mod_rows, d), mod_map),
        pl.BlockSpec((1, mod_rows, d), mod_map),
        pl.BlockSpec((1, mod_rows, d), mod_map),
        pl.BlockSpec((1, dv), lambda i: (0, 0)),
        pl.BlockSpec((v_w + sb_w, d), lambda i: (0, 0)),
        pl.BlockSpec((1, d), lambda i: (0, 0)),
        pl.BlockSpec((1, d), lambda i: (0, 0)),
        pl.BlockSpec((n_e, d), lambda i: (0, 0)),
        pl.BlockSpec((n_e, d), lambda i: (0, 0)),
    ]
    args = [oa, p_arr, ob, x2d, gate1, shift2, scale2, g_gla, w_out, g_post, g_pre_ffn, wr_hi, wr_lo]
    aliases = {}
    if shared is not None:
        in_specs += [pl.BlockSpec(memory_space=pl.ANY), pl.BlockSpec(memory_space=pl.ANY)]
        aliases = {len(args): 1, len(args) + 1: 2}
        args += list(shared)
        kernel = functools.partial(_drop_args, kernel, 13, 2)
    return pl.pallas_call(
        kernel,
        out_shape=(
            jax.ShapeDtypeStruct((m, d), f32),
            jax.ShapeDtypeStruct((n_all, d), f32),
            jax.ShapeDtypeStruct((n_e, n_all), f32),
        ),
        grid=(n_steps,),
        in_specs=in_specs,
        out_specs=(
            pl.BlockSpec((tm, d), lambda i: (row(i), 0)),
            pl.BlockSpec((tm, d), lambda i: (i + off, 0)),
            pl.BlockSpec((n_e, tm), lambda i: (0, i + off)),
        ),
        input_output_aliases=aliases,
        compiler_params=_params(("arbitrary",)),
        name="mixout",
    )(*args)


def _drop_args(kernel, start, count, *refs):
    return kernel(*refs[:start], *refs[start + count:])


def _route_kernel(lg_ref, b_ref, e_ref, g_ref, *, n_experts):
    per = n_experts // N_GROUPS
    assert per == SUBLANES
    neg = -jnp.inf
    t = lg_ref.shape[1]
    sub = lax.broadcasted_iota(jnp.int32, (per, t), 0)
    scores, biased = [], []
    for g in range(N_GROUPS):
        s = 1.0 / (1.0 + jnp.exp(-lg_ref[g * per:(g + 1) * per, :]))
        scores.append(s)
        biased.append(s + b_ref[g * per:(g + 1) * per, :1])

    def col_max(x):
        return jnp.broadcast_to(jnp.max(x, axis=0, keepdims=True), x.shape)

    def col_min(x):
        return jnp.broadcast_to(jnp.min(x, axis=0, keepdims=True), x.shape)

    def col_sum(x):
        return jnp.broadcast_to(jnp.sum(x, axis=0, keepdims=True), x.shape)

    grp = []
    for g in range(N_GROUPS):
        m1 = col_max(biased[g])
        first = col_min(jnp.where(biased[g] == m1, sub, per))
        m2 = col_max(jnp.where(sub == first, neg, biased[g]))
        grp.append(m1 + m2)
    chosen = [jnp.zeros((per, t), f32) for _ in range(N_GROUPS)]
    for _ in range(TOPK_GROUPS):
        best = grp[0]
        for g in range(1, N_GROUPS):
            best = jnp.maximum(best, grp[g])
        free = jnp.ones((per, t), f32)
        for g in range(N_GROUPS):
            pick = jnp.where(grp[g] == best, free, 0.0)
            free = free - pick
            chosen[g] = chosen[g] + pick
            grp[g] = jnp.where(pick > 0.0, neg, grp[g])
    cand = [jnp.where(chosen[g] > 0.0, biased[g], neg) for g in range(N_GROUPS)]
    e_out = jnp.zeros((TOP_K, t), jnp.int32)
    g_out = jnp.zeros((TOP_K, t), f32)
    assert TOP_K == per
    for k in range(TOP_K):
        best = col_max(cand[0])
        for g in range(1, N_GROUPS):
            best = jnp.maximum(best, col_max(cand[g]))
        sel = jnp.full((per, t), n_experts, jnp.int32)
        for g in range(N_GROUPS):
            sel = jnp.minimum(sel, col_min(jnp.where(cand[g] == best, sub + g * per, n_experts)))
        gate = jnp.zeros((per, t), f32)
        for g in range(N_GROUPS):
            hit = (sub + g * per) == sel
            gate = gate + col_sum(jnp.where(hit, scores[g], 0.0))
            cand[g] = jnp.where(hit, neg, cand[g])
        e_out = jnp.where(sub == k, sel, e_out)
        g_out = jnp.where(sub == k, gate, g_out)
    e_ref[...] = e_out
    g_ref[...] = g_out / col_sum(g_out) * ROUTED_SCALE


def _route(lg_all, b_router):
    n_e, n = lg_all.shape
    t = _pick_tile(n, (512, 384, 256, 128))
    b = jnp.broadcast_to(b_router.astype(f32)[:, None], (n_e, LANES))
    kernel = functools.partial(_route_kernel, n_experts=n_e)
    return pl.pallas_call(
        kernel,
        out_shape=(jax.ShapeDtypeStruct((TOP_K, n), jnp.int32), jax.ShapeDtypeStruct((TOP_K, n), f32)),
        grid=(n // t,),
        in_specs=[pl.BlockSpec((n_e, t), lambda i: (0, i)), pl.BlockSpec((n_e, LANES), lambda i: (0, 0))],
        out_specs=(pl.BlockSpec((TOP_K, t), lambda i: (0, i)), pl.BlockSpec((TOP_K, t), lambda i: (0, i))),
        compiler_params=_params(("arbitrary",)),
        name="route",
    )(lg_all, b)


def _moe_kernel(blk_e_ref, nact_ref, tok_ref, h_hbm, wg_ref, wu_ref, wd_ref, y_ref,
                x_buf, wg_b, wu_b, wd_b, sem, *, tm):
    t = pl.program_id(0)
    active = t < nact_ref[0]

    def row_copy(r):
        return pltpu.make_async_copy(h_hbm.at[pl.ds(tok_ref[0, 0, r], 1)], x_buf.at[pl.ds(r, 1)], sem)

    @pl.when(active)
    def _():
        def start(r, _):
            row_copy(r).start()
            return 0
        lax.fori_loop(0, tm, start, 0)

    prev = blk_e_ref[jnp.maximum(t - 1, 0)]
    new_expert = jnp.logical_or(t == 0, blk_e_ref[t] != prev)

    @pl.when(jnp.logical_and(active, new_expert))
    def _():
        wg_b[...] = wg_ref[0].astype(bf16)
        wu_b[...] = wu_ref[0].astype(bf16)
        wd_b[...] = wd_ref[0].astype(bf16)

    @pl.when(active)
    def _():
        def wait(r, _):
            row_copy(r).wait()
            return 0
        lax.fori_loop(0, tm, wait, 0)
        x = x_buf[...].astype(bf16)
        g = _dot(x, wg_b[...])
        u = _dot(x, wu_b[...])
        a = (_silu(g) * u).astype(bf16)
        y_ref[...] = _dot(a, wd_b[...])

    @pl.when(jnp.logical_not(active))
    def _():
        y_ref[...] = jnp.zeros_like(y_ref)


def _moe(h_all, blk_e, n_active, row_tok, w_g, w_u, w_d):
    n_tiles = blk_e.shape[0]
    tm = MOE_TM
    d = h_all.shape[1]
    de = w_g.shape[2]
    kernel = functools.partial(_moe_kernel, tm=tm)
    return pl.pallas_call(
        kernel,
        out_shape=jax.ShapeDtypeStruct((n_tiles * tm, d), f32),
        grid_spec=pltpu.PrefetchScalarGridSpec(
            num_scalar_prefetch=2,
            grid=(n_tiles,),
            in_specs=[
                pl.BlockSpec((1, 1, tm), lambda t, be, na: (t, 0, 0), memory_space=pltpu.SMEM),
                pl.BlockSpec(memory_space=pl.ANY),
                pl.BlockSpec((1, d, de), lambda t, be, na: (be[t], 0, 0)),
                pl.BlockSpec((1, d, de), lambda t, be, na: (be[t], 0, 0)),
                pl.BlockSpec((1, de, d), lambda t, be, na: (be[t], 0, 0)),
            ],
            out_specs=pl.BlockSpec((tm, d), lambda t, be, na: (t, 0)),
            scratch_shapes=[
                pltpu.VMEM((tm, d), f32),
                pltpu.VMEM((d, de), bf16),
                pltpu.VMEM((d, de), bf16),
                pltpu.VMEM((de, d), bf16),
                pltpu.SemaphoreType.DMA,
            ],
        ),
        compiler_params=_params(("arbitrary",)),
        name="moe_gemm",
    )(blk_e, n_active, row_tok.reshape(n_tiles, 1, tm), h_all, w_g, w_u, w_d)


def _combine_kernel(dest_ref, y_hbm, rg_ref, h_ref, x1_ref, gate2_ref, wsg_ref, wsu_ref, wsd_ref, gpost_ref,
                    o_ref, y_buf, sem, *, tm, d_model):
    def row_copy(k, r):
        return pltpu.make_async_copy(y_hbm.at[pl.ds(dest_ref[0, k, r], 1)], y_buf.at[k, pl.ds(r, 1)], sem)

    def start(r, _):
        for k in range(TOP_K):
            row_copy(k, r).start()
        return 0

    lax.fori_loop(0, tm, start, 0)
    h = h_ref[...].astype(bf16)
    a = (_silu(_dot(h, wsg_ref[...])) * _dot(h, wsu_ref[...])).astype(bf16)
    f = _dot(a, wsd_ref[...])

    def wait(r, _):
        for k in range(TOP_K):
            row_copy(k, r).wait()
        return 0

    lax.fori_loop(0, tm, wait, 0)
    rg = rg_ref[...]
    routed = y_buf[0] * rg[:, 0:1]
    for k in range(1, TOP_K):
        routed = routed + y_buf[k] * rg[:, k:k + 1]
    f = f + routed
    r = lax.rsqrt(jnp.sum(f * f, axis=-1, keepdims=True) * (1.0 / d_model) + EPS)
    o_ref[...] = x1_ref[...] + gate2_ref[0] * (f * r * gpost_ref[...])


def _combine(dest, y_rows, gates_wide, h_all, row_off, x1, gate2, rows_per_mod, ws_g, ws_u, ws_d, g_post, *, tm):
    m, d = x1.shape
    ds = ws_g.shape[1]
    mod_rows = gate2.shape[1]
    tiles_per_mod = rows_per_mod // tm if mod_rows == 1 else 1
    off = row_off // tm
    assert row_off % tm == 0
    kernel = functools.partial(_combine_kernel, tm=tm, d_model=d)
    return pl.pallas_call(
        kernel,
        out_shape=jax.ShapeDtypeStruct((m, d), f32),
        grid=(m // tm,),
        in_specs=[
            pl.BlockSpec((1, TOP_K, tm), lambda i: (i + off, 0, 0), memory_space=pltpu.SMEM),
            pl.BlockSpec(memory_space=pl.ANY),
            pl.BlockSpec((tm, LANES), lambda i: (i + off, 0)),
            pl.BlockSpec((tm, d), lambda i: (i + off, 0)),
            pl.BlockSpec((tm, d), lambda i: (i, 0)),
            pl.BlockSpec((1, mod_rows, d), lambda i: (i // tiles_per_mod, 0, 0)),
            pl.BlockSpec((d, ds), lambda i: (0, 0)),
            pl.BlockSpec((d, ds), lambda i: (0, 0)),
            pl.BlockSpec((ds, d), lambda i: (0, 0)),
            pl.BlockSpec((1, d), lambda i: (0, 0)),
        ],
        out_specs=pl.BlockSpec((tm, d), lambda i: (i, 0)),
        scratch_shapes=[pltpu.VMEM((TOP_K, tm, d), f32), pltpu.SemaphoreType.DMA],
        compiler_params=_params(("arbitrary",)),
        name="combine",
    )(dest, y_rows, gates_wide, h_all, x1, gate2, ws_g, ws_u, ws_d, g_post)


def _layer(xp, xs, c_prompt, c_sample, cache_k, cache_v, state0, page_table, layer,
           w_ada, b_ada, g_pre_mix, g_post_mix, g_pre_ffn, g_post_ffn, w_in, w_gla_a2, b_gla_a,
           g_gla_norm, b_sb, w_out, w_router, b_router, w_e_gate, w_e_up, w_e_down, w_s_gate, w_s_up, w_s_down):
    batch, seq, d = xp.shape
    db, dseq, _ = xs.shape
    _, gla_heads, dk, dv = state0.shape
    sb_heads, dh = cache_k.shape[3:]
    rank = w_gla_a2.shape[0]
    n_e = w_router.shape[1]
    qk_w, v_w, sb_w = gla_heads * dk, gla_heads * dv, sb_heads * dh
    n_p_rows, n_s_rows = batch * seq, db * dseq
    n_tok = n_p_rows + n_s_rows
    tn = 512
    assert qk_w % tn == 0 and v_w % tn == 0 and sb_w % tn == 0 and v_w % qk_w == 0 and rank <= LANES

    m_rows = batch + db
    m_pad = -(-m_rows // SUBLANES) * SUBLANES
    c_all = jnp.concatenate([c_prompt, c_sample, jnp.zeros((m_pad - m_rows, d), f32)], 0)
    mod = _modulation(c_all, w_ada, b_ada)
    mod_p = [mod[:batch, i * d:(i + 1) * d].reshape(batch, 1, d) for i in range(6)]
    mod_s = [jnp.repeat(mod[batch:m_rows, i * d:(i + 1) * d], dseq, axis=0).reshape(1, n_s_rows, d)
             for i in range(6)]

    base = 2 * qk_w + 2 * v_w
    w_main = jnp.concatenate([w_in[:, :base], w_in[:, base + rank:]], axis=1).astype(bf16)
    w_ra = jnp.pad(w_in[:, base:base + rank], ((0, 0), (0, LANES - rank))).astype(bf16)
    w_a2 = jnp.pad(w_gla_a2, ((0, LANES - rank), (0, 0))).astype(bf16)
    b_a = b_gla_a.reshape(1, qk_w)
    w_out_b = w_out.astype(bf16)
    wr_t = w_router.T
    wr_hi = wr_t.astype(bf16)
    wr_lo = (wr_t - wr_hi.astype(f32)).astype(bf16)
    g_pre_mix2, g_post_mix2 = g_pre_mix.reshape(1, d), g_post_mix.reshape(1, d)
    g_pre_ffn2, g_post_ffn2 = g_pre_ffn.reshape(1, d), g_post_ffn.reshape(1, d)
    g_gla2 = g_gla_norm.reshape(1, dv)

    k_col, v_col, ga_col = 1, (2 * qk_w) // v_w, (2 * qk_w + v_w) // v_w
    la_col = (2 * qk_w + 2 * v_w + sb_w) // qk_w
    qb_col0 = (2 * qk_w + 2 * v_w) // dh

    xp2 = xp.reshape(n_p_rows, d)
    proj_p, kb_p, vb_p = _inproj(xp2, mod_p[0], mod_p[1], seq, g_pre_mix2, w_main, w_ra, w_a2, b_a,
                                 tm=1024, tn=tn, qk_w=qk_w, v_w=v_w, sb_w=sb_w)
    oa_p, state_p = _gla(proj_p, proj_p, proj_p, proj_p, (0, k_col, v_col, la_col),
                         jnp.zeros((batch, gla_heads, dk, dv), f32),
                         batch=batch, n_chunks=seq // GLA_CHUNK, chunk=GLA_CHUNK, n_valid=GLA_CHUNK,
                         heads=gla_heads, dk=dk, dv=dv)
    ob_p = _sb_prompt(proj_p, qb_col0, kb_p, vb_p, b_sb, batch=batch, seq=seq, heads=sb_heads, dh=dh)

    xs2 = xs.reshape(n_s_rows, d)
    proj_s, kb_s, vb_s = _inproj(xs2, mod_s[0], mod_s[1], n_s_rows, g_pre_mix2, w_main, w_ra, w_a2, b_a,
                                 tm=n_s_rows, tn=tn, qk_w=qk_w, v_w=v_w, sb_w=sb_w)
    pad_seq = GLA_CHUNK

    def pad_tokens(a):
        w = a.shape[1]
        return jnp.pad(a.reshape(db, dseq, w), ((0, 0), (0, pad_seq - dseq), (0, 0))).reshape(db * pad_seq, w)

    oa_s_pad, state_s = _gla(pad_tokens(proj_s[:, :qk_w]), pad_tokens(proj_s[:, qk_w:2 * qk_w]),
                             pad_tokens(proj_s[:, 2 * qk_w:2 * qk_w + v_w]),
                             pad_tokens(proj_s[:, la_col * qk_w:(la_col + 1) * qk_w]), (0, 0, 0, 0), state0,
                             batch=db, n_chunks=1, chunk=pad_seq, n_valid=dseq, heads=gla_heads, dk=dk, dv=dv)
    oa_s = oa_s_pad.reshape(db, pad_seq, v_w)[:, :dseq].reshape(n_s_rows, v_w)

    q_rows = proj_s[:, qb_col0 * dh:qb_col0 * dh + sb_w].reshape(db, dseq, sb_heads, dh)
    q_rows = q_rows.transpose(0, 2, 1, 3).reshape(db, sb_heads * dseq, dh)
    bias_rows = jnp.broadcast_to(jnp.repeat(b_sb.astype(f32), dseq)[:, None], (sb_heads * dseq, LANES))

    def pad_new(a):
        a = jnp.pad(a.reshape(db, dseq, sb_heads, dh), ((0, 0), (0, NEW_POS_PAD - dseq), (0, 0), (0, 0)))
        return a.reshape(db, NEW_POS_PAD * sb_heads, dh)

    ob_rows = _sb_sample(q_rows, bias_rows, pad_new(kb_s), pad_new(vb_s), cache_k, cache_v, layer,
                         page_table, heads=sb_heads, n_q=dseq, dh=dh)
    ob_s = ob_rows.reshape(db, sb_heads, dseq, dh).transpose(0, 2, 1, 3).reshape(n_s_rows, sb_w)

    x1_p, h2_all, lg_all = _mixout(oa_p, proj_p, ga_col, ob_p, xp2, mod_p[2], mod_p[3], mod_p[4], seq, g_gla2,
                                   w_out_b, g_post_mix2, g_pre_ffn2, wr_hi, wr_lo, None, n_tok, 0,
                                   tm=256, heads=gla_heads, dv=dv)
    x1_s, h2_all, lg_all = _mixout(oa_s, proj_s, ga_col, ob_s, xs2, mod_s[2], mod_s[3], mod_s[4], n_s_rows, g_gla2,
                                   w_out_b, g_post_mix2, g_pre_ffn2, wr_hi, wr_lo, (h2_all, lg_all), n_tok,
                                   n_p_rows, tm=n_s_rows, heads=gla_heads, dv=dv)

    top_e, gates = _route(lg_all, b_router)
    n_asg = TOP_K * n_tok
    flat_e = top_e.reshape(n_asg)
    flat_tok = jnp.tile(jnp.arange(n_tok, dtype=jnp.int32), TOP_K)
    order = jnp.argsort(flat_e)
    e_sorted = flat_e[order]
    counts = jnp.bincount(flat_e, length=n_e)
    padded = (counts + MOE_TM - 1) // MOE_TM * MOE_TM
    pad_end = jnp.cumsum(padded)
    pad_start = pad_end - padded
    start = jnp.cumsum(counts) - counts
    dest_sorted = (pad_start[e_sorted] + jnp.arange(n_asg) - start[e_sorted]).astype(jnp.int32)
    n_tiles = -(-n_asg // MOE_TM) + n_e
    row_tok = jnp.zeros((n_tiles * MOE_TM,), jnp.int32).at[dest_sorted].set(flat_tok[order])
    dest = jnp.zeros((n_asg,), jnp.int32).at[order].set(dest_sorted)
    blk_e = jnp.minimum(jnp.searchsorted(pad_end, jnp.arange(n_tiles) * MOE_TM, side='right'), n_e - 1)
    n_active = (pad_end[-1] // MOE_TM).reshape(1).astype(jnp.int32)
    y_rows = _moe(h2_all, blk_e.astype(jnp.int32), n_active, row_tok, w_e_gate, w_e_up, w_e_down)

    tm_c = LANES
    assert n_p_rows % tm_c == 0 and n_s_rows % tm_c == 0
    dest3 = dest.reshape(TOP_K, n_tok // tm_c, tm_c).transpose(1, 0, 2)
    gates_wide = jnp.pad(gates.T, ((0, 0), (0, LANES - TOP_K)))
    ws = (w_s_gate.astype(bf16), w_s_up.astype(bf16), w_s_down.astype(bf16))
    yp = _combine(dest3, y_rows, gates_wide, h2_all, 0, x1_p, mod_p[5], seq, *ws, g_post_ffn2, tm=tm_c)
    ys = _combine(dest3, y_rows, gates_wide, h2_all, n_p_rows, x1_s, mod_s[5], n_s_rows, *ws, g_post_ffn2, tm=tm_c)
    return (yp.reshape(batch, seq, d), ys.reshape(db, dseq, d),
            kb_p.reshape(batch, seq, sb_heads, dh), vb_p.reshape(batch, seq, sb_heads, dh), state_p,
            kb_s.reshape(db, dseq, sb_heads, dh), vb_s.reshape(db, dseq, sb_heads, dh), state_s)


def kernel(x_prompt, x_sample, c_prompt, c_sample, cache_sb_k, cache_sb_v, state_gla, page_table, w_ada, b_ada, g_pre_mix, g_post_mix, g_pre_ffn, g_post_ffn, w_in, w_gla_a2, b_gla_a, g_gla_norm, b_sb, w_out, w_router, b_router, w_e_gate, w_e_up, w_e_down, w_s_gate, w_s_up, w_s_down):
    xp, xs = x_prompt, x_sample
    outs = [[] for _ in range(6)]
    for layer in range(w_in.shape[0]):
        res = _layer(xp, xs, c_prompt, c_sample, cache_sb_k, cache_sb_v, state_gla[layer], page_table, layer,
                     w_ada[layer], b_ada[layer], g_pre_mix[layer], g_post_mix[layer], g_pre_ffn[layer],
                     g_post_ffn[layer], w_in[layer], w_gla_a2[layer], b_gla_a[layer], g_gla_norm[layer],
                     b_sb[layer], w_out[layer], w_router[layer], b_router[layer], w_e_gate[layer], w_e_up[layer],
                     w_e_down[layer], w_s_gate[layer], w_s_up[layer], w_s_down[layer])
        xp, xs = res[0], res[1]
        for acc, val in zip(outs, res[2:]):
            acc.append(val)
    return (xp, xs) + tuple(jnp.stack(o) for o in outs)
```
